```python
import math
import jax, jax.numpy as jnp
from jax import lax
import numpy as np

D_MODEL = 1024
BATCH = 8
SEQ = 2048
DEPTH = 4
DEC_BATCH = 4
DEC_SEQ = 8192
PAST_LEN = 128

EPS = 1e-6
N_BRANCH = 3
D_CONV = 768
CONV_K = 31
D_SG = 768
SG_CHUNK = 128
SG_GROUPS = 6
SG_GROUP_DIM = D_SG // SG_GROUPS
N_HEADS = 12
QK_NOPE = 64
QK_ROPE = 32
QK_HEAD = QK_NOPE + QK_ROPE
V_HEAD = 64
Q_LORA = 256
KV_LORA = 128
ROPE_BASE = 10000.0
Q_BLOCK = 128
D_FF = ((8 * D_MODEL // 3 + 255) // 256) * 256
D_IN = N_BRANCH * D_MODEL + 2 * D_CONV + 2 * D_SG + Q_LORA + KV_LORA + QK_ROPE

kernel_name = 'hybrid_conv_gmlp_mla_encoder'


def rmsnorm(x, g):
    xf = x.astype(jnp.float32)
    y = xf * lax.rsqrt(jnp.mean(xf * xf, axis=-1, keepdims=True) + EPS)
    return (y * g.astype(jnp.float32)).astype(x.dtype)


def layernorm(x, g, b):
    xf = x.astype(jnp.float32)
    mu = jnp.mean(xf, axis=-1, keepdims=True)
    var = jnp.mean(jnp.square(xf - mu), axis=-1, keepdims=True)
    y = (xf - mu) * lax.rsqrt(var + EPS)
    return (y * g.astype(jnp.float32) + b.astype(jnp.float32)).astype(x.dtype)


def rope_tables(S, dtype):
    pos = jnp.arange(S, dtype=jnp.float32)
    inv = ROPE_BASE ** (-jnp.arange(0, QK_ROPE, 2, dtype=jnp.float32) / QK_ROPE)
    ang = pos[:, None] * inv[None, :]
    return jnp.cos(ang).astype(dtype), jnp.sin(ang).astype(dtype)


def apply_rope(x, cos, sin):
    c = cos[None, :, None, :]
    s = sin[None, :, None, :]
    x1, x2 = jnp.split(x, 2, axis=-1)
    return jnp.concatenate([x1 * c - x2 * s, x2 * c + x1 * s], axis=-1)


def conv_branch(h, conv_w, conv_b, conv_norm_g, conv_norm_b, w_conv_out):
    a, gt = jnp.split(h, 2, axis=-1)
    z = a * jax.nn.sigmoid(gt)
    z = lax.conv_general_dilated(
        z, conv_w[:, None, :], window_strides=(1,),
        padding=((CONV_K // 2, CONV_K // 2),),
        dimension_numbers=('NWC', 'WIO', 'NWC'),
        feature_group_count=D_CONV) + conv_b
    z = jax.nn.silu(layernorm(z, conv_norm_g, conv_norm_b))
    return z @ w_conv_out


def sg_branch(h, sg_norm_g, w_spatial, b_spatial, w_sg_out):
    B, S, _ = h.shape
    u, v = jnp.split(jax.nn.gelu(h), 2, axis=-1)
    v = rmsnorm(v, sg_norm_g)
    vb = v.reshape(B, S // SG_CHUNK, SG_CHUNK, SG_GROUPS, SG_GROUP_DIM)
    sv = jnp.einsum('gpq,bnqgc->bnpgc', w_spatial, vb) + b_spatial.T[:, :, None]
    return (u * sv.reshape(B, S, D_SG)) @ w_sg_out


def mla_branch(q_lat, kv_lat, k_pe, cos, sin, q_norm_g, w_uq, kv_norm_g, w_ukv,
               qk_q_g, qk_k_g, w_o):
    B, S, _ = q_lat.shape
    q = (rmsnorm(q_lat, q_norm_g) @ w_uq).reshape(B, S, N_HEADS, QK_HEAD)
    kv = (rmsnorm(kv_lat, kv_norm_g) @ w_ukv).reshape(B, S, N_HEADS, QK_NOPE + V_HEAD)
    k_nope, v = jnp.split(kv, [QK_NOPE], axis=-1)
    k = jnp.concatenate(
        [k_nope, jnp.broadcast_to(k_pe[:, :, None, :], (B, S, N_HEADS, QK_ROPE))], axis=-1)
    q = rmsnorm(q, qk_q_g)
    k = rmsnorm(k, qk_k_g)
    q = jnp.concatenate([q[..., :QK_NOPE], apply_rope(q[..., QK_NOPE:], cos, sin)], axis=-1)
    k = jnp.concatenate([k[..., :QK_NOPE], apply_rope(k[..., QK_NOPE:], cos, sin)], axis=-1)
    q = q * (1.0 / math.sqrt(QK_HEAD))
    q = q.transpose(0, 2, 1, 3)
    k = k.transpose(0, 2, 1, 3)
    v = v.transpose(0, 2, 1, 3)
    nb = S // Q_BLOCK
    qb = q.reshape(B, N_HEADS, nb, Q_BLOCK, QK_HEAD).transpose(2, 0, 1, 3, 4)

    def attend(qblk):
        s = jnp.einsum('bhqd,bhkd->bhqk', qblk, k).astype(jnp.float32)
        p = jax.nn.softmax(s, axis=-1)
        return jnp.einsum('bhqk,bhkv->bhqv', p.astype(v.dtype), v)

    o = lax.map(attend, qb)
    o = o.transpose(1, 0, 3, 2, 4).reshape(B, S, N_HEADS * V_HEAD)
    return o @ w_o


def encoder_layer(x, cos, sin, ln1_g, w_in, b_gate, conv_w, conv_b, conv_norm_g, conv_norm_b,
                  w_conv_out, sg_norm_g, w_spatial, b_spatial, w_sg_out, q_norm_g, w_uq,
                  kv_norm_g, w_ukv, qk_q_g, qk_k_g, w_o, w_out, ln2_g, w_ffn_in, w_ffn_out):
    B, S, _ = x.shape
    h = rmsnorm(x, ln1_g) @ w_in
    cuts = [N_BRANCH * D_MODEL,
            N_BRANCH * D_MODEL + 2 * D_CONV,
            N_BRANCH * D_MODEL + 2 * D_CONV + 2 * D_SG,
            N_BRANCH * D_MODEL + 2 * D_CONV + 2 * D_SG + Q_LORA,
            N_BRANCH * D_MODEL + 2 * D_CONV + 2 * D_SG + Q_LORA + KV_LORA]
    g_lin, h_conv, h_sg, q_lat, kv_lat, k_pe = jnp.split(h, cuts, axis=-1)
    gates = jax.nn.sigmoid(g_lin + b_gate).reshape(B, S, N_BRANCH, D_MODEL)
    y_a = conv_branch(h_conv, conv_w, conv_b, conv_norm_g, conv_norm_b, w_conv_out)
    y_b = sg_branch(h_sg, sg_norm_g, w_spatial, b_spatial, w_sg_out)
    y_c = mla_branch(q_lat, kv_lat, k_pe, cos, sin, q_norm_g, w_uq, kv_norm_g, w_ukv,
                     qk_q_g, qk_k_g, w_o)
    merged = gates[:, :, 0] * y_a + gates[:, :, 1] * y_b + gates[:, :, 2] * y_c
    x = x + merged @ w_out
    f_in, f_gate = jnp.split(rmsnorm(x, ln2_g) @ w_ffn_in, 2, axis=-1)
    x = x + (jax.nn.silu(f_gate) * f_in) @ w_ffn_out
    return x


def trunk(x, params):
    cos, sin = rope_tables(x.shape[1], x.dtype)
    for l in range(DEPTH):
        x = encoder_layer(x, cos, sin, *[p[l] for p in params])
    return x


def setup_inputs(seed: int = 0) -> dict:
    key = jax.random.key(seed)
    ks = jax.random.split(key, 32)

    def nrm(k, shape, scale):
        return jax.random.normal(k, shape, dtype=jnp.float32) * scale

    def gain(k, shape):
        return 1.0 + 0.02 * jax.random.normal(k, shape, dtype=jnp.float32)

    L = DEPTH
    return {
        'x_prompt': nrm(ks[0], (BATCH, SEQ, D_MODEL), 1.0),
        'x_sample': nrm(ks[1], (DEC_BATCH, DEC_SEQ, D_MODEL), 1.0),
        'ln1_g': gain(ks[2], (L, D_MODEL)),
        'w_in': nrm(ks[3], (L, D_MODEL, D_IN), D_MODEL ** -0.5),
        'b_gate': nrm(ks[4], (L, N_BRANCH * D_MODEL), 0.02),
        'conv_w': nrm(ks[5], (L, CONV_K, D_CONV), CONV_K ** -0.5),
        'conv_b': nrm(ks[6], (L, D_CONV), 0.02),
        'conv_norm_g': gain(ks[7], (L, D_CONV)),
        'conv_norm_b': nrm(ks[8], (L, D_CONV), 0.02),
        'w_conv_out': nrm(ks[9], (L, D_CONV, D_MODEL), D_CONV ** -0.5),
        'sg_norm_g': gain(ks[10], (L, D_SG)),
        'w_spatial': nrm(ks[11], (L, SG_GROUPS, SG_CHUNK, SG_CHUNK), SG_CHUNK ** -0.5),
        'b_spatial': gain(ks[12], (L, SG_GROUPS, SG_CHUNK)),
        'w_sg_out': nrm(ks[13], (L, D_SG, D_MODEL), D_SG ** -0.5),
        'q_norm_g': gain(ks[14], (L, Q_LORA)),
        'w_uq': nrm(ks[15], (L, Q_LORA, N_HEADS * QK_HEAD), Q_LORA ** -0.5),
        'kv_norm_g': gain(ks[16], (L, KV_LORA)),
        'w_ukv': nrm(ks[17], (L, KV_LORA, N_HEADS * (QK_NOPE + V_HEAD)), KV_LORA ** -0.5),
        'qk_q_g': gain(ks[18], (L, QK_HEAD)),
        'qk_k_g': gain(ks[19], (L, QK_HEAD)),
        'w_o': nrm(ks[20], (L, N_HEADS * V_HEAD, D_MODEL), (N_HEADS * V_HEAD) ** -0.5),
        'w_out': nrm(ks[21], (L, D_MODEL, D_MODEL), D_MODEL ** -0.5),
        'ln2_g': gain(ks[22], (L, D_MODEL)),
        'w_ffn_in': nrm(ks[23], (L, D_MODEL, 2 * D_FF), D_MODEL ** -0.5),
        'w_ffn_out': nrm(ks[24], (L, D_FF, D_MODEL), D_FF ** -0.5),
    }


def reference(x_prompt, x_sample, ln1_g, w_in, b_gate, conv_w, conv_b, conv_norm_g, conv_norm_b,
              w_conv_out, sg_norm_g, w_spatial, b_spatial, w_sg_out, q_norm_g, w_uq, kv_norm_g,
              w_ukv, qk_q_g, qk_k_g, w_o, w_out, ln2_g, w_ffn_in, w_ffn_out):
    params = (ln1_g, w_in, b_gate, conv_w, conv_b, conv_norm_g, conv_norm_b, w_conv_out,
              sg_norm_g, w_spatial, b_spatial, w_sg_out, q_norm_g, w_uq, kv_norm_g, w_ukv,
              qk_q_g, qk_k_g, w_o, w_out, ln2_g, w_ffn_in, w_ffn_out)
    y_prompt = trunk(x_prompt, params)
    y_sample = trunk(x_sample, params)
    return (y_prompt, y_sample)
```

```python
import math
from functools import partial

import numpy as np
import jax
import jax.numpy as jnp
from jax import lax
from jax.experimental import pallas as pl
from jax.experimental.pallas import tpu as pltpu

F32 = jnp.float32
BF16 = jnp.bfloat16

EPS = 1e-6
D_MODEL = 1024
D_CONV = 768
CONV_K = 31
D_SG = 768
SG_CHUNK = 128
SG_GROUPS = 6
N_HEADS = 12
QK_NOPE = 64
QK_ROPE = 32
QK_HEAD = QK_NOPE + QK_ROPE
V_HEAD = 64
Q_LORA = 256
KV_LORA = 128
ROPE_BASE = 10000.0
D_FF = 2816

LANES = 128
HEAD_PAD = LANES
HALO = 16
D_QK_PAD = N_HEADS * HEAD_PAD

OFF_G = 0
OFF_C = OFF_G + 3 * D_MODEL
OFF_S = OFF_C + 2 * D_CONV
OFF_Q = OFF_S + 2 * D_SG
OFF_KV = OFF_Q + Q_LORA
OFF_KPE = OFF_KV + KV_LORA
D_IN_PAD = OFF_KPE + LANES

TM_PRE = 256
TM_POST = 256
TQ_ATTN = 256
VMEM_LIMIT = 56 * 1024 * 1024

_HALF = QK_ROPE // 2
_HEAD_SRC = np.concatenate([
    np.arange(QK_NOPE, QK_NOPE + _HALF),
    np.arange(0, 48),
    np.arange(QK_NOPE + _HALF, QK_HEAD),
    np.arange(48, QK_NOPE),
])


def _rms(x, g):
    ms = jnp.mean(x * x, axis=-1, keepdims=True)
    return x * lax.rsqrt(ms + EPS) * g


def _gelu_tanh(x):
    c = math.sqrt(2.0 / math.pi)
    return 0.5 * x * (1.0 + jnp.tanh(c * (x + 0.044715 * (x * x * x))))


def _dot(a, b):
    return jnp.dot(a, b, preferred_element_type=F32)


def _dot_nt(a, b):
    return lax.dot_general(a, b, (((1,), (1,)), ((), ())), preferred_element_type=F32)


def _pre_kernel(x_ref, xp_ref, xn_ref, cs_ref, sn_ref, ln1_ref, win_ref, bg_ref, cw_ref,
                cb_ref, cng_ref, cnb_ref, wco_ref, sgg_ref, wsp_ref, bsp_ref, wso_ref,
                qng_ref, wuq_ref, kvg_ref, wuk_ref, wuv_ref, gq_ref, gk_ref,
                mab_ref, g2_ref, q_ref, k_ref, vt_ref, zs_ref):
    tm = x_ref.shape[0]
    j = pl.program_id(1)
    nj = pl.num_programs(1)
    ln1 = ln1_ref[...]

    xn = _rms(x_ref[...], ln1).astype(BF16)
    xh = jnp.concatenate([xp_ref[...], xn_ref[...]], axis=0)
    xhn = _rms(xh, ln1).astype(BF16)
    xe = jnp.concatenate([xhn[:HALO], xn, xhn[HALO:]], axis=0)

    hc = _dot(xe, win_ref[:, OFF_C:OFF_C + 2 * D_CONV])
    z = hc[:, :D_CONV] * jax.nn.sigmoid(hc[:, D_CONV:])
    row = lax.broadcasted_iota(jnp.int32, (tm + 2 * HALO, 1), 0)
    inside = jnp.logical_and(jnp.logical_or(row >= HALO, j > 0),
                             jnp.logical_or(row < tm + HALO, j < nj - 1))
    zs_ref[...] = jnp.where(inside, z, 0.0)
    acc = jnp.broadcast_to(cb_ref[...], (tm, D_CONV))
    for t in range(CONV_K):
        off = HALO - CONV_K // 2 + t
        acc = acc + cw_ref[t:t + 1, :] * zs_ref[off:off + tm, :]
    mu = jnp.mean(acc, axis=-1, keepdims=True)
    cen = acc - mu
    var = jnp.mean(cen * cen, axis=-1, keepdims=True)
    yn = cen * lax.rsqrt(var + EPS) * cng_ref[...] + cnb_ref[...]
    ya = _dot((yn * jax.nn.sigmoid(yn)).astype(BF16), wco_ref[...])

    g0 = jax.nn.sigmoid(_dot(xn, win_ref[:, OFF_G:OFF_G + D_MODEL]) + bg_ref[:, 0:D_MODEL])
    mab = g0 * ya

    hs = _gelu_tanh(_dot(xn, win_ref[:, OFF_S:OFF_S + 2 * D_SG]))
    u = hs[:, :D_SG]
    v = _rms(hs[:, D_SG:], sgg_ref[...]).astype(BF16)
    chunks = []
    for c in range(tm // SG_CHUNK):
        cols = []
        for g in range(SG_GROUPS):
            vb = v[c * SG_CHUNK:(c + 1) * SG_CHUNK, g * LANES:(g + 1) * LANES]
            cols.append(_dot(wsp_ref[g], vb))
        chunks.append(jnp.concatenate(cols, axis=1) + bsp_ref[...])
    sv = jnp.concatenate(chunks, axis=0)
    yb = _dot((u * sv).astype(BF16), wso_ref[...])
    g1 = jax.nn.sigmoid(_dot(xn, win_ref[:, OFF_G + D_MODEL:OFF_G + 2 * D_MODEL])
                        + bg_ref[:, D_MODEL:2 * D_MODEL])
    mab_ref[...] = mab + g1 * yb

    g2 = jax.nn.sigmoid(_dot(xn, win_ref[:, OFF_G + 2 * D_MODEL:OFF_G + 3 * D_MODEL])
                        + bg_ref[:, 2 * D_MODEL:3 * D_MODEL])
    g2_ref[...] = g2.astype(g2_ref.dtype)

    cs = cs_ref[...]
    sn = sn_ref[...]

    def head_norm_rope(hraw, g):
        ms = jnp.sum(hraw * hraw, axis=-1, keepdims=True) * (1.0 / QK_HEAD)
        hn = hraw * lax.rsqrt(ms + EPS) * g
        return hn * cs + pltpu.roll(hn, HEAD_PAD // 2, 1) * sn

    ql = _rms(_dot(xn, win_ref[:, OFF_Q:OFF_Q + Q_LORA]), qng_ref[...]).astype(BF16)
    q_raw = _dot(ql, wuq_ref[...])
    gq = gq_ref[...]
    scale = 1.0 / math.sqrt(QK_HEAD)
    for h in range(N_HEADS):
        qh = head_norm_rope(q_raw[:, h * HEAD_PAD:(h + 1) * HEAD_PAD], gq)
        q_ref[:, h * HEAD_PAD:(h + 1) * HEAD_PAD] = (qh * scale).astype(q_ref.dtype)

    kvl = _rms(_dot(xn, win_ref[:, OFF_KV:OFF_KV + KV_LORA]), kvg_ref[...]).astype(BF16)
    k_nope = _dot(kvl, wuk_ref[...])
    k_pe = _dot(xn, win_ref[:, OFF_KPE:OFF_KPE + LANES])
    gk = gk_ref[...]
    for h in range(N_HEADS):
        kh = head_norm_rope(k_nope[:, h * HEAD_PAD:(h + 1) * HEAD_PAD] + k_pe, gk)
        k_ref[:, h * HEAD_PAD:(h + 1) * HEAD_PAD] = kh.astype(k_ref.dtype)

    vv = _dot(kvl, wuv_ref[...])
    vt_ref[...] = vv.T.astype(vt_ref.dtype)


def _const_spec(shape):
    nd = len(shape)
    return pl.BlockSpec(shape, lambda b, j: (0,) * nd, pipeline_mode=pl.Buffered(1))


def _pre_call(x, rope_c, rope_s, p):
    B, S, D = x.shape
    tm = TM_PRE
    nj = S // tm
    hb = tm // HALO
    last_hb = S // HALO - 1
    in_specs = [
        pl.BlockSpec((None, tm, D), lambda b, j: (b, j, 0)),
        pl.BlockSpec((None, HALO, D), lambda b, j: (b, jnp.maximum(j * hb - 1, 0), 0)),
        pl.BlockSpec((None, HALO, D), lambda b, j: (b, jnp.minimum((j + 1) * hb, last_hb), 0)),
        pl.BlockSpec((tm, LANES), lambda b, j: (j, 0)),
        pl.BlockSpec((tm, LANES), lambda b, j: (j, 0)),
    ]
    consts = [p['ln1_g'], p['w_in'], p['b_gate'], p['conv_w'], p['conv_b'], p['conv_norm_g'],
              p['conv_norm_b'], p['w_conv_out'], p['sg_norm_g'], p['w_spatial'], p['b_spatial'],
              p['w_sg_out'], p['q_norm_g'], p['w_uq'], p['kv_norm_g'], p['w_uk'], p['w_uv'],
              p['g_q'], p['g_k']]
    in_specs += [_const_spec(c.shape) for c in consts]
    out_shape = [
        jax.ShapeDtypeStruct((B, S, D), F32),
        jax.ShapeDtypeStruct((B, S, D), BF16),
        jax.ShapeDtypeStruct((B, S, D_QK_PAD), BF16),
        jax.ShapeDtypeStruct((B, S, D_QK_PAD), BF16),
        jax.ShapeDtypeStruct((B, N_HEADS * V_HEAD, S), BF16),
    ]
    out_specs = [
        pl.BlockSpec((None, tm, D), lambda b, j: (b, j, 0)),
        pl.BlockSpec((None, tm, D), lambda b, j: (b, j, 0)),
        pl.BlockSpec((None, tm, D_QK_PAD), lambda b, j: (b, j, 0)),
        pl.BlockSpec((None, tm, D_QK_PAD), lambda b, j: (b, j, 0)),
        pl.BlockSpec((None, N_HEADS * V_HEAD, tm), lambda b, j: (b, 0, j)),
    ]
    return pl.pallas_call(
        _pre_kernel,
        grid=(B, nj),
        in_specs=in_specs,
        out_specs=out_specs,
        out_shape=out_shape,
        scratch_shapes=[pltpu.VMEM((tm + 2 * HALO, D_CONV), F32)],
        compiler_params=pltpu.CompilerParams(
            dimension_semantics=("parallel", "parallel"), vmem_limit_bytes=VMEM_LIMIT),
        name="pre_mixer",
    )(x, x, x, rope_c, rope_s, *consts)


def _attn_kernel(q_ref, k_ref, vt_ref, o_ref):
    S = k_ref.shape[0]
    ones = jnp.ones((HALO, S), BF16)
    outs = []
    for a in range(2):
        qh = q_ref[:, a * HEAD_PAD:(a + 1) * HEAD_PAD]
        kh = k_ref[:, a * HEAD_PAD:(a + 1) * HEAD_PAD]
        st = _dot_nt(kh, qh)
        m = jnp.max(st, axis=0, keepdims=True)
        pexp = jnp.exp(st - m).astype(BF16)
        vaug = jnp.concatenate([vt_ref[a * V_HEAD:(a + 1) * V_HEAD, :], ones], axis=0)
        ot = _dot(vaug, pexp)
        outs.append(ot[:V_HEAD, :] / ot[V_HEAD:V_HEAD + 1, :])
    o_ref[...] = jnp.concatenate(outs, axis=0).T.astype(o_ref.dtype)


def _attn_call(q, k, vt):
    B, S, _ = q.shape
    tq = TQ_ATTN
    return pl.pallas_call(
        _attn_kernel,
        grid=(B, N_HEADS // 2, S // tq),
        in_specs=[
            pl.BlockSpec((None, tq, 2 * HEAD_PAD), lambda b, h, i: (b, i, h)),
            pl.BlockSpec((None, S, 2 * HEAD_PAD), lambda b, h, i: (b, 0, h)),
            pl.BlockSpec((None, 2 * V_HEAD, S), lambda b, h, i: (b, h, 0)),
        ],
        out_specs=pl.BlockSpec((None, tq, 2 * V_HEAD), lambda b, h, i: (b, i, h)),
        out_shape=jax.ShapeDtypeStruct((B, S, N_HEADS * V_HEAD), BF16),
        compiler_params=pltpu.CompilerParams(
            dimension_semantics=("parallel", "parallel", "arbitrary"), vmem_limit_bytes=VMEM_LIMIT),
        name="attention",
    )(q, k, vt)


def _post_kernel(x_ref, o_ref, mab_ref, g2_ref, wo_ref, wout_ref, ln2_ref, wfi_ref, wfo_ref, y_ref):
    yc = _dot(o_ref[...], wo_ref[...])
    merged = mab_ref[...] + g2_ref[...].astype(F32) * yc
    x1 = x_ref[...] + _dot(merged.astype(BF16), wout_ref[...])
    hn = _rms(x1, ln2_ref[...]).astype(BF16)
    f = _dot(hn, wfi_ref[...])
    f_in = f[:, :D_FF]
    f_gate = f[:, D_FF:]
    act = (f_gate * jax.nn.sigmoid(f_gate) * f_in).astype(BF16)
    y_ref[...] = x1 + _dot(act, wfo_ref[...])


def _post_call(x, o, mab, g2, p):
    T, D = x.shape
    tm = TM_POST
    consts = [p['w_o'], p['w_out'], p['ln2_g'], p['w_ffn_in'], p['w_ffn_out']]

    def cspec(shape):
        nd = len(shape)
        return pl.BlockSpec(shape, lambda i: (0,) * nd, pipeline_mode=pl.Buffered(1))

    return pl.pallas_call(
        _post_kernel,
        grid=(T // tm,),
        in_specs=[
            pl.BlockSpec((tm, D), lambda i: (i, 0)),
            pl.BlockSpec((tm, N_HEADS * V_HEAD), lambda i: (i, 0)),
            pl.BlockSpec((tm, D), lambda i: (i, 0)),
            pl.BlockSpec((tm, D), lambda i: (i, 0)),
        ] + [cspec(c.shape) for c in consts],
        out_specs=pl.BlockSpec((tm, D), lambda i: (i, 0)),
        out_shape=jax.ShapeDtypeStruct((T, D), F32),
        compiler_params=pltpu.CompilerParams(
            dimension_semantics=("parallel",), vmem_limit_bytes=VMEM_LIMIT),
        name="post_mixer",
    )(x, o, mab, g2, *consts)


def _prep_layer(l, ln1_g, w_in, b_gate, conv_w, conv_b, conv_norm_g, conv_norm_b, w_conv_out,
                sg_norm_g, w_spatial, b_spatial, w_sg_out, q_norm_g, w_uq, kv_norm_g, w_ukv,
                qk_q_g, qk_k_g, w_o, w_out, ln2_g, w_ffn_in, w_ffn_out):
    src = jnp.asarray(_HEAD_SRC)
    wi = w_in[l]
    d_in = wi.shape[1]
    kpe = wi[:, d_in - QK_ROPE:]
    kpe_pad = jnp.zeros((D_MODEL, LANES), F32)
    kpe_pad = kpe_pad.at[:, 0:_HALF].set(kpe[:, :_HALF])
    kpe_pad = kpe_pad.at[:, HEAD_PAD // 2:HEAD_PAD // 2 + _HALF].set(kpe[:, _HALF:])
    w_in_p = jnp.concatenate([wi[:, :d_in - QK_ROPE], kpe_pad], axis=1).astype(BF16)

    def pad_heads(w):
        w = jnp.pad(w, ((0, 0), (0, 0), (0, HEAD_PAD - QK_HEAD)))
        return w.reshape(w.shape[0], N_HEADS * HEAD_PAD)

    wuq = pad_heads(w_uq[l].reshape(Q_LORA, N_HEADS, QK_HEAD)[:, :, src]).astype(BF16)
    wkv = w_ukv[l].reshape(KV_LORA, N_HEADS, QK_NOPE + V_HEAD)
    wk = jnp.concatenate([wkv[:, :, :QK_NOPE], jnp.zeros((KV_LORA, N_HEADS, QK_ROPE), F32)], axis=2)
    wuk = pad_heads(wk[:, :, src]).astype(BF16)
    wuv = wkv[:, :, QK_NOPE:].reshape(KV_LORA, N_HEADS * V_HEAD).astype(BF16)

    def pad_gain(g):
        return jnp.pad(g[src], (0, HEAD_PAD - QK_HEAD)).reshape(1, HEAD_PAD)

    return {
        'ln1_g': ln1_g[l].reshape(1, -1),
        'w_in': w_in_p,
        'b_gate': b_gate[l].reshape(1, -1),
        'conv_w': conv_w[l],
        'conv_b': conv_b[l].reshape(1, -1),
        'conv_norm_g': conv_norm_g[l].reshape(1, -1),
        'conv_norm_b': conv_norm_b[l].reshape(1, -1),
        'w_conv_out': w_conv_out[l].astype(BF16),
        'sg_norm_g': sg_norm_g[l].reshape(1, -1),
        'w_spatial': w_spatial[l].astype(BF16),
        'b_spatial': jnp.repeat(b_spatial[l].T, D_SG // SG_GROUPS, axis=1),
        'w_sg_out': w_sg_out[l].astype(BF16),
        'q_norm_g': q_norm_g[l].reshape(1, -1),
        'w_uq': wuq,
        'kv_norm_g': kv_norm_g[l].reshape(1, -1),
        'w_uk': wuk,
        'w_uv': wuv,
        'g_q': pad_gain(qk_q_g[l]),
        'g_k': pad_gain(qk_k_g[l]),
        'w_o': w_o[l].astype(BF16),
        'w_out': w_out[l].astype(BF16),
        'ln2_g': ln2_g[l].reshape(1, -1),
        'w_ffn_in': w_ffn_in[l].astype(BF16),
        'w_ffn_out': w_ffn_out[l].astype(BF16),
    }


def _rope_tables(S):
    pos = jnp.arange(S, dtype=F32)
    inv = ROPE_BASE ** (-jnp.arange(0, QK_ROPE, 2, dtype=F32) / QK_ROPE)
    ang = pos[:, None] * inv[None, :]
    cos = jnp.cos(ang)
    sin = jnp.sin(ang)
    c = jnp.ones((S, HEAD_PAD), F32)
    c = c.at[:, 0:_HALF].set(cos).at[:, HEAD_PAD // 2:HEAD_PAD // 2 + _HALF].set(cos)
    s = jnp.zeros((S, HEAD_PAD), F32)
    s = s.at[:, 0:_HALF].set(-sin).at[:, HEAD_PAD // 2:HEAD_PAD // 2 + _HALF].set(sin)
    return c, s


def _trunk(x, layers):
    B, S, D = x.shape
    rope_c, rope_s = _rope_tables(S)
    for p in layers:
        mab, g2, q, k, vt = _pre_call(x, rope_c, rope_s, p)
        o = _attn_call(q, k, vt)
        x = _post_call(x.reshape(B * S, D), o.reshape(B * S, -1), mab.reshape(B * S, D),
                       g2.reshape(B * S, D), p).reshape(B, S, D)
    return x


def kernel(x_prompt, x_sample, ln1_g, w_in, b_gate, conv_w, conv_b, conv_norm_g, conv_norm_b, w_conv_out, sg_norm_g, w_spatial, b_spatial, w_sg_out, q_norm_g, w_uq, kv_norm_g, w_ukv, qk_q_g, qk_k_g, w_o, w_out, ln2_g, w_ffn_in, w_ffn_out):
    params = (ln1_g, w_in, b_gate, conv_w, conv_b, conv_norm_g, conv_norm_b, w_conv_out,
              sg_norm_g, w_spatial, b_spatial, w_sg_out, q_norm_g, w_uq, kv_norm_g, w_ukv,
              qk_q_g, qk_k_g, w_o, w_out, ln2_g, w_ffn_in, w_ffn_out)
    layers = [_prep_layer(l, *params) for l in range(ln1_g.shape[0])]
    return (_trunk(x_prompt, layers), _trunk(x_sample, layers))
```

```python
import math
from functools import partial

import numpy as np
import jax
import jax.numpy as jnp
from jax import lax
from jax.experimental import pallas as pl
from jax.experimental.pallas import tpu as pltpu

F32 = jnp.float32
BF16 = jnp.bfloat16

EPS = 1e-6
D_MODEL = 1024
D_CONV = 768
CONV_K = 31
D_SG = 768
SG_CHUNK = 128
SG_GROUPS = 6
N_HEADS = 12
QK_NOPE = 64
QK_ROPE = 32
QK_HEAD = QK_NOPE + QK_ROPE
V_HEAD = 64
Q_LORA = 256
KV_LORA = 128
ROPE_BASE = 10000.0
D_FF = 2816

LANES = 128
HEAD_PAD = LANES
HALO = 16
D_QK_PAD = N_HEADS * HEAD_PAD

OFF_G = 0
OFF_C = OFF_G + 3 * D_MODEL
OFF_S = OFF_C + 2 * D_CONV
OFF_Q = OFF_S + 2 * D_SG
OFF_KV = OFF_Q + Q_LORA
OFF_KPE = OFF_KV + KV_LORA
D_IN_PAD = OFF_KPE + LANES

TM_PRE = 256
TM_POST = 256
TQ_ATTN = 256
ATTN_CHUNK = 512
ATTN_HEADS_PER_STEP = 6
VMEM_LIMIT = 56 * 1024 * 1024

_HALF = QK_ROPE // 2
_HEAD_SRC = np.concatenate([
    np.arange(QK_NOPE, QK_NOPE + _HALF),
    np.arange(0, 48),
    np.arange(QK_NOPE + _HALF, QK_HEAD),
    np.arange(48, QK_NOPE),
])


def _rms(x, g):
    ms = jnp.mean(x * x, axis=-1, keepdims=True)
    return x * lax.rsqrt(ms + EPS) * g


def _gelu_tanh(x):
    c = math.sqrt(2.0 / math.pi)
    return 0.5 * x * (1.0 + jnp.tanh(c * (x + 0.044715 * (x * x * x))))


def _dot(a, b):
    return jnp.dot(a, b, preferred_element_type=F32)


def _dot_nt(a, b):
    return lax.dot_general(a, b, (((1,), (1,)), ((), ())), preferred_element_type=F32)


def _pre_kernel(x_ref, xp_ref, xn_ref, cs_ref, sn_ref, ln1_ref, win_ref, bg_ref, cw_ref,
                cb_ref, cng_ref, cnb_ref, wco_ref, sgg_ref, wsp_ref, bsp_ref, wso_ref,
                qng_ref, wuq_ref, kvg_ref, wuk_ref, wuv_ref, gq_ref, gk_ref,
                mab_ref, g2_ref, qt_ref, k_ref, vt_ref, zs_ref):
    tm = x_ref.shape[0]
    j = pl.program_id(1)
    nj = pl.num_programs(1)
    ln1 = ln1_ref[...]

    xn = _rms(x_ref[...], ln1).astype(BF16)
    xh = jnp.concatenate([xp_ref[...], xn_ref[...]], axis=0)
    xhn = _rms(xh, ln1).astype(BF16)
    xe = jnp.concatenate([xhn[:HALO], xn, xhn[HALO:]], axis=0)

    hc = _dot(xe, win_ref[:, OFF_C:OFF_C + 2 * D_CONV])
    z = hc[:, :D_CONV] * jax.nn.sigmoid(hc[:, D_CONV:])
    row = lax.broadcasted_iota(jnp.int32, (tm + 2 * HALO, 1), 0)
    inside = jnp.logical_and(jnp.logical_or(row >= HALO, j > 0),
                             jnp.logical_or(row < tm + HALO, j < nj - 1))
    zs_ref[...] = jnp.where(inside, z, 0.0)
    acc = jnp.broadcast_to(cb_ref[...], (tm, D_CONV))
    for t in range(CONV_K):
        off = HALO - CONV_K // 2 + t
        acc = acc + cw_ref[t:t + 1, :] * zs_ref[off:off + tm, :]
    mu = jnp.mean(acc, axis=-1, keepdims=True)
    cen = acc - mu
    var = jnp.mean(cen * cen, axis=-1, keepdims=True)
    yn = cen * lax.rsqrt(var + EPS) * cng_ref[...] + cnb_ref[...]
    ya = _dot((yn * jax.nn.sigmoid(yn)).astype(BF16), wco_ref[...])

    g0 = jax.nn.sigmoid(_dot(xn, win_ref[:, OFF_G:OFF_G + D_MODEL]) + bg_ref[:, 0:D_MODEL])
    mab = g0 * ya

    hs = _gelu_tanh(_dot(xn, win_ref[:, OFF_S:OFF_S + 2 * D_SG]))
    u = hs[:, :D_SG]
    v = _rms(hs[:, D_SG:], sgg_ref[...]).astype(BF16)
    chunks = []
    for c in range(tm // SG_CHUNK):
        cols = []
        for g in range(SG_GROUPS):
            vb = v[c * SG_CHUNK:(c + 1) * SG_CHUNK, g * LANES:(g + 1) * LANES]
            cols.append(_dot(wsp_ref[g], vb))
        chunks.append(jnp.concatenate(cols, axis=1) + bsp_ref[...])
    sv = jnp.concatenate(chunks, axis=0)
    yb = _dot((u * sv).astype(BF16), wso_ref[...])
    g1 = jax.nn.sigmoid(_dot(xn, win_ref[:, OFF_G + D_MODEL:OFF_G + 2 * D_MODEL])
                        + bg_ref[:, D_MODEL:2 * D_MODEL])
    mab_ref[...] = mab + g1 * yb

    g2 = jax.nn.sigmoid(_dot(xn, win_ref[:, OFF_G + 2 * D_MODEL:OFF_G + 3 * D_MODEL])
                        + bg_ref[:, 2 * D_MODEL:3 * D_MODEL])
    g2_ref[...] = g2.astype(g2_ref.dtype)

    cs = cs_ref[...]
    sn = sn_ref[...]

    def head_norm_rope(hraw, g):
        ms = jnp.sum(hraw * hraw, axis=-1, keepdims=True) * (1.0 / QK_HEAD)
        hn = hraw * lax.rsqrt(ms + EPS) * g
        return hn * cs + pltpu.roll(hn, HEAD_PAD // 2, 1) * sn

    ql = _rms(_dot(xn, win_ref[:, OFF_Q:OFF_Q + Q_LORA]), qng_ref[...]).astype(BF16)
    q_raw = _dot(ql, wuq_ref[...])
    gq = gq_ref[...]
    scale = math.log2(math.e) / math.sqrt(QK_HEAD)
    for h in range(N_HEADS):
        qh = head_norm_rope(q_raw[:, h * HEAD_PAD:(h + 1) * HEAD_PAD], gq)
        qt_ref[h * HEAD_PAD:(h + 1) * HEAD_PAD, :] = (qh * scale).T.astype(qt_ref.dtype)

    kvl = _rms(_dot(xn, win_ref[:, OFF_KV:OFF_KV + KV_LORA]), kvg_ref[...]).astype(BF16)
    k_nope = _dot(kvl, wuk_ref[...])
    k_pe = _dot(xn, win_ref[:, OFF_KPE:OFF_KPE + LANES])
    gk = gk_ref[...]
    for h in range(N_HEADS):
        kh = head_norm_rope(k_nope[:, h * HEAD_PAD:(h + 1) * HEAD_PAD] + k_pe, gk)
        k_ref[h] = kh.astype(k_ref.dtype)

    vv = _dot(kvl, wuv_ref[...])
    vt_ref[...] = vv.T.astype(vt_ref.dtype)


def _const_spec(shape):
    nd = len(shape)
    return pl.BlockSpec(shape, lambda b, j: (0,) * nd, pipeline_mode=pl.Buffered(1))


def _pre_call(x, rope_c, rope_s, p):
    B, S, D = x.shape
    tm = TM_PRE
    nj = S // tm
    hb = tm // HALO
    last_hb = S // HALO - 1
    in_specs = [
        pl.BlockSpec((None, tm, D), lambda b, j: (b, j, 0)),
        pl.BlockSpec((None, HALO, D), lambda b, j: (b, jnp.maximum(j * hb - 1, 0), 0)),
        pl.BlockSpec((None, HALO, D), lambda b, j: (b, jnp.minimum((j + 1) * hb, last_hb), 0)),
        pl.BlockSpec((tm, LANES), lambda b, j: (j, 0)),
        pl.BlockSpec((tm, LANES), lambda b, j: (j, 0)),
    ]
    consts = [p['ln1_g'], p['w_in'], p['b_gate'], p['conv_w'], p['conv_b'], p['conv_norm_g'],
              p['conv_norm_b'], p['w_conv_out'], p['sg_norm_g'], p['w_spatial'], p['b_spatial'],
              p['w_sg_out'], p['q_norm_g'], p['w_uq'], p['kv_norm_g'], p['w_uk'], p['w_uv'],
              p['g_q'], p['g_k']]
    in_specs += [_const_spec(c.shape) for c in consts]
    out_shape = [
        jax.ShapeDtypeStruct((B, S, D), F32),
        jax.ShapeDtypeStruct((B, S, D), BF16),
        jax.ShapeDtypeStruct((B, D_QK_PAD, S), BF16),
        jax.ShapeDtypeStruct((B, N_HEADS, S, HEAD_PAD), BF16),
        jax.ShapeDtypeStruct((B, N_HEADS * V_HEAD, S), BF16),
    ]
    out_specs = [
        pl.BlockSpec((None, tm, D), lambda b, j: (b, j, 0)),
        pl.BlockSpec((None, tm, D), lambda b, j: (b, j, 0)),
        pl.BlockSpec((None, D_QK_PAD, tm), lambda b, j: (b, 0, j)),
        pl.BlockSpec((None, N_HEADS, tm, HEAD_PAD), lambda b, j: (b, 0, j, 0)),
        pl.BlockSpec((None, N_HEADS * V_HEAD, tm), lambda b, j: (b, 0, j)),
    ]
    return pl.pallas_call(
        _pre_kernel,
        grid=(B, nj),
        in_specs=in_specs,
        out_specs=out_specs,
        out_shape=out_shape,
        scratch_shapes=[pltpu.VMEM((tm + 2 * HALO, D_CONV), F32)],
        compiler_params=pltpu.CompilerParams(
            dimension_semantics=("parallel", "parallel"), vmem_limit_bytes=VMEM_LIMIT),
        name="pre_mixer",
    )(x, x, x, rope_c, rope_s, *consts)


def _attn_kernel(qt_ref, k_ref, vt_ref, o_ref, s_scr, ot_scr):
    G, S = k_ref.shape[0], k_ref.shape[1]
    tq = qt_ref.shape[2]
    ck = min(ATTN_CHUNK, S)
    nck = S // ck
    ones = jnp.ones((HALO, ck), BF16)
    neg = jnp.full((8, tq), -jnp.inf, F32)

    def produce(h, c, m8):
        st = _dot(k_ref[h, c * ck:(c + 1) * ck, :], qt_ref[h])
        s_scr[c * ck:(c + 1) * ck, :] = st
        r = st
        while r.shape[0] > 8:
            half = r.shape[0] // 2
            r = jnp.maximum(r[:half], r[half:])
        return jnp.maximum(m8, r)

    def consume(h, c, m, acc):
        p = jnp.exp2(s_scr[c * ck:(c + 1) * ck, :] - m).astype(BF16)
        vaug = jnp.concatenate([vt_ref[h, :, c * ck:(c + 1) * ck], ones], axis=0)
        return acc + _dot(vaug, p)

    def finish(h, acc):
        row = pl.multiple_of(h * V_HEAD, V_HEAD)
        ot_scr[pl.ds(row, V_HEAD), :] = acc[:V_HEAD] / acc[V_HEAD:V_HEAD + 1]

    def row_max(m8):
        return jnp.max(m8, axis=0, keepdims=True)

    m8 = neg
    for c in range(nck):
        m8 = produce(0, c, m8)

    def body(h, m_prev):
        acc = jnp.zeros((V_HEAD + HALO, tq), F32)
        m8 = neg
        for c in range(nck):
            acc = consume(h - 1, c, m_prev, acc)
            m8 = produce(h, c, m8)
        finish(h - 1, acc)
        return row_max(m8)

    m_last = lax.fori_loop(1, G, body, row_max(m8))
    acc = jnp.zeros((V_HEAD + HALO, tq), F32)
    for c in range(nck):
        acc = consume(G - 1, c, m_last, acc)
    finish(G - 1, acc)
    o_ref[...] = ot_scr[...].T.astype(o_ref.dtype)


def _attn_call(qt, k, vt):
    B, H, S, _ = k.shape
    tq = TQ_ATTN
    G = ATTN_HEADS_PER_STEP
    single = pl.Buffered(1)
    return pl.pallas_call(
        _attn_kernel,
        grid=(B, H // G, S // tq),
        in_specs=[
            pl.BlockSpec((None, G, HEAD_PAD, tq), lambda b, g, i: (b, g, 0, i)),
            pl.BlockSpec((None, G, S, HEAD_PAD), lambda b, g, i: (b, g, 0, 0), pipeline_mode=single),
            pl.BlockSpec((None, G, V_HEAD, S), lambda b, g, i: (b, g, 0, 0), pipeline_mode=single),
        ],
        out_specs=pl.BlockSpec((None, tq, G * V_HEAD), lambda b, g, i: (b, i, g)),
        out_shape=jax.ShapeDtypeStruct((B, S, H * V_HEAD), BF16),
        scratch_shapes=[pltpu.VMEM((S, tq), F32), pltpu.VMEM((G * V_HEAD, tq), F32)],
        compiler_params=pltpu.CompilerParams(
            dimension_semantics=("parallel", "parallel", "arbitrary"), vmem_limit_bytes=VMEM_LIMIT),
        name="attention",
    )(qt, k, vt)


def _post_kernel(x_ref, o_ref, mab_ref, g2_ref, wo_ref, wout_ref, ln2_ref, wfi_ref, wfo_ref, y_ref):
    yc = _dot(o_ref[...], wo_ref[...])
    merged = mab_ref[...] + g2_ref[...].astype(F32) * yc
    x1 = x_ref[...] + _dot(merged.astype(BF16), wout_ref[...])
    hn = _rms(x1, ln2_ref[...]).astype(BF16)
    f = _dot(hn, wfi_ref[...])
    f_in = f[:, :D_FF]
    f_gate = f[:, D_FF:]
    act = (f_gate * jax.nn.sigmoid(f_gate) * f_in).astype(BF16)
    y_ref[...] = x1 + _dot(act, wfo_ref[...])


def _post_call(x, o, mab, g2, p):
    T, D = x.shape
    tm = TM_POST
    consts = [p['w_o'], p['w_out'], p['ln2_g'], p['w_ffn_in'], p['w_ffn_out']]

    def cspec(shape):
        nd = len(shape)
        return pl.BlockSpec(shape, lambda i: (0,) * nd, pipeline_mode=pl.Buffered(1))

    return pl.pallas_call(
        _post_kernel,
        grid=(T // tm,),
        in_specs=[
            pl.BlockSpec((tm, D), lambda i: (i, 0)),
            pl.BlockSpec((tm, N_HEADS * V_HEAD), lambda i: (i, 0)),
            pl.BlockSpec((tm, D), lambda i: (i, 0)),
            pl.BlockSpec((tm, D), lambda i: (i, 0)),
        ] + [cspec(c.shape) for c in consts],
        out_specs=pl.BlockSpec((tm, D), lambda i: (i, 0)),
        out_shape=jax.ShapeDtypeStruct((T, D), F32),
        compiler_params=pltpu.CompilerParams(
            dimension_semantics=("parallel",), vmem_limit_bytes=VMEM_LIMIT),
        name="post_mixer",
    )(x, o, mab, g2, *consts)


def _prep_layer(l, ln1_g, w_in, b_gate, conv_w, conv_b, conv_norm_g, conv_norm_b, w_conv_out,
                sg_norm_g, w_spatial, b_spatial, w_sg_out, q_norm_g, w_uq, kv_norm_g, w_ukv,
                qk_q_g, qk_k_g, w_o, w_out, ln2_g, w_ffn_in, w_ffn_out):
    src = jnp.asarray(_HEAD_SRC)
    wi = w_in[l]
    d_in = wi.shape[1]
    kpe = wi[:, d_in - QK_ROPE:]
    kpe_pad = jnp.zeros((D_MODEL, LANES), F32)
    kpe_pad = kpe_pad.at[:, 0:_HALF].set(kpe[:, :_HALF])
    kpe_pad = kpe_pad.at[:, HEAD_PAD // 2:HEAD_PAD // 2 + _HALF].set(kpe[:, _HALF:])
    w_in_p = jnp.concatenate([wi[:, :d_in - QK_ROPE], kpe_pad], axis=1).astype(BF16)

    def pad_heads(w):
        w = jnp.pad(w, ((0, 0), (0, 0), (0, HEAD_PAD - QK_HEAD)))
        return w.reshape(w.shape[0], N_HEADS * HEAD_PAD)

    wuq = pad_heads(w_uq[l].reshape(Q_LORA, N_HEADS, QK_HEAD)[:, :, src]).astype(BF16)
    wkv = w_ukv[l].reshape(KV_LORA, N_HEADS, QK_NOPE + V_HEAD)
    wk = jnp.concatenate([wkv[:, :, :QK_NOPE], jnp.zeros((KV_LORA, N_HEADS, QK_ROPE), F32)], axis=2)
    wuk = pad_heads(wk[:, :, src]).astype(BF16)
    wuv = wkv[:, :, QK_NOPE:].reshape(KV_LORA, N_HEADS * V_HEAD).astype(BF16)

    def pad_gain(g):
        return jnp.pad(g[src], (0, HEAD_PAD - QK_HEAD)).reshape(1, HEAD_PAD)

    return {
        'ln1_g': ln1_g[l].reshape(1, -1),
        'w_in': w_in_p,
        'b_gate': b_gate[l].reshape(1, -1),
        'conv_w': conv_w[l],
        'conv_b': conv_b[l].reshape(1, -1),
        'conv_norm_g': conv_norm_g[l].reshape(1, -1),
        'conv_norm_b': conv_norm_b[l].reshape(1, -1),
        'w_conv_out': w_conv_out[l].astype(BF16),
        'sg_norm_g': sg_norm_g[l].reshape(1, -1),
        'w_spatial': w_spatial[l].astype(BF16),
        'b_spatial': jnp.repeat(b_spatial[l].T, D_SG // SG_GROUPS, axis=1),
        'w_sg_out': w_sg_out[l].astype(BF16),
        'q_norm_g': q_norm_g[l].reshape(1, -1),
        'w_uq': wuq,
        'kv_norm_g': kv_norm_g[l].reshape(1, -1),
        'w_uk': wuk,
        'w_uv': wuv,
        'g_q': pad_gain(qk_q_g[l]),
        'g_k': pad_gain(qk_k_g[l]),
        'w_o': w_o[l].astype(BF16),
        'w_out': w_out[l].astype(BF16),
        'ln2_g': ln2_g[l].reshape(1, -1),
        'w_ffn_in': w_ffn_in[l].astype(BF16),
        'w_ffn_out': w_ffn_out[l].astype(BF16),
    }


def _rope_tables(S):
    pos = jnp.arange(S, dtype=F32)
    inv = ROPE_BASE ** (-jnp.arange(0, QK_ROPE, 2, dtype=F32) / QK_ROPE)
    ang = pos[:, None] * inv[None, :]
    cos = jnp.cos(ang)
    sin = jnp.sin(ang)
    c = jnp.ones((S, HEAD_PAD), F32)
    c = c.at[:, 0:_HALF].set(cos).at[:, HEAD_PAD // 2:HEAD_PAD // 2 + _HALF].set(cos)
    s = jnp.zeros((S, HEAD_PAD), F32)
    s = s.at[:, 0:_HALF].set(-sin).at[:, HEAD_PAD // 2:HEAD_PAD // 2 + _HALF].set(sin)
    return c, s


def _trunk(x, layers):
    B, S, D = x.shape
    rope_c, rope_s = _rope_tables(S)
    for p in layers:
        mab, g2, qt, k, vt = _pre_call(x, rope_c, rope_s, p)
        o = _attn_call(qt.reshape(B, N_HEADS, HEAD_PAD, S), k, vt.reshape(B, N_HEADS, V_HEAD, S))
        x = _post_call(x.reshape(B * S, D), o.reshape(B * S, -1), mab.reshape(B * S, D),
                       g2.reshape(B * S, D), p).reshape(B, S, D)
    return x


def kernel(x_prompt, x_sample, ln1_g, w_in, b_gate, conv_w, conv_b, conv_norm_g, conv_norm_b, w_conv_out, sg_norm_g, w_spatial, b_spatial, w_sg_out, q_norm_g, w_uq, kv_norm_g, w_ukv, qk_q_g, qk_k_g, w_o, w_out, ln2_g, w_ffn_in, w_ffn_out):
    params = (ln1_g, w_in, b_gate, conv_w, conv_b, conv_norm_g, conv_norm_b, w_conv_out,
              sg_norm_g, w_spatial, b_spatial, w_sg_out, q_norm_g, w_uq, kv_norm_g, w_ukv,
              qk_q_g, qk_k_g, w_o, w_out, ln2_g, w_ffn_in, w_ffn_out)
    layers = [_prep_layer(l, *params) for l in range(ln1_g.shape[0])]
    return (_trunk(x_prompt, layers), _trunk(x_sample, layers))
```

```python
import math
from functools import partial

import numpy as np
import jax
import jax.numpy as jnp
from jax import lax
from jax.experimental import pallas as pl
from jax.experimental.pallas import tpu as pltpu

F32 = jnp.float32
BF16 = jnp.bfloat16

EPS = 1e-6
D_MODEL = 1024
D_CONV = 768
CONV_K = 31
D_SG = 768
SG_CHUNK = 128
SG_GROUPS = 6
N_HEADS = 12
QK_NOPE = 64
QK_ROPE = 32
QK_HEAD = QK_NOPE + QK_ROPE
V_HEAD = 64
Q_LORA = 256
KV_LORA = 128
ROPE_BASE = 10000.0
D_FF = 2816

LANES = 128
SUBLANES = 8
HEAD_PAD = LANES
HALO = 16
D_QK_PAD = N_HEADS * HEAD_PAD

OFF_G = 0
OFF_C = OFF_G + 3 * D_MODEL
OFF_S = OFF_C + 2 * D_CONV
OFF_Q = OFF_S + 2 * D_SG
OFF_KV = OFF_Q + Q_LORA
OFF_KPE = OFF_KV + KV_LORA
D_IN_PAD = OFF_KPE + LANES

TM_PRE = 256
TM_POST = 256
TQ_ATTN = 256
ATTN_CHUNK = 512
ATTN_HEADS_PER_STEP = 12
VMEM_LIMIT = 56 * 1024 * 1024

_HALF = QK_ROPE // 2
_HEAD_SRC = np.concatenate([
    np.arange(QK_NOPE, QK_NOPE + _HALF),
    np.arange(0, 48),
    np.arange(QK_NOPE + _HALF, QK_HEAD),
    np.arange(48, QK_NOPE),
])


def _rms(x, g):
    ms = jnp.mean(x * x, axis=-1, keepdims=True)
    return x * lax.rsqrt(ms + EPS) * g


def _gelu_tanh(x):
    c = math.sqrt(2.0 / math.pi)
    return 0.5 * x * (1.0 + jnp.tanh(c * (x + 0.044715 * (x * x * x))))


def _dot(a, b):
    return jnp.dot(a, b, preferred_element_type=F32)


def _dot_nt(a, b):
    return lax.dot_general(a, b, (((1,), (1,)), ((), ())), preferred_element_type=F32)


def _pre_kernel(x_ref, xp_ref, xn_ref, cs_ref, sn_ref, ln1_ref, win_ref, bg_ref, cw_ref,
                cb_ref, cng_ref, cnb_ref, wco_ref, sgg_ref, wsp_ref, bsp_ref, wso_ref,
                qng_ref, wuq_ref, kvg_ref, wuk_ref, wuv_ref, gq_ref, gk_ref,
                mab_ref, g2_ref, qt_ref, k_ref, vt_ref, zs_ref, zsh_ref):
    tm = x_ref.shape[0]
    j = pl.program_id(1)
    nj = pl.num_programs(1)
    ln1 = ln1_ref[...]

    xn = _rms(x_ref[...], ln1).astype(BF16)
    xh = jnp.concatenate([xp_ref[...], xn_ref[...]], axis=0)
    xhn = _rms(xh, ln1).astype(BF16)
    xe = jnp.concatenate([xhn[:HALO], xn, xhn[HALO:]], axis=0)

    hc = _dot(xe, win_ref[:, OFF_C:OFF_C + 2 * D_CONV])
    z = hc[:, :D_CONV] * jax.nn.sigmoid(hc[:, D_CONV:])
    row = lax.broadcasted_iota(jnp.int32, (tm + 2 * HALO, 1), 0)
    inside = jnp.logical_and(jnp.logical_or(row >= HALO, j > 0),
                             jnp.logical_or(row < tm + HALO, j < nj - 1))
    zs_ref[...] = jnp.where(inside, z, 0.0)
    span = tm + SUBLANES * (2 * HALO // SUBLANES - 1)
    for b in range(1, SUBLANES):
        zsh_ref[b - 1] = zs_ref[b:b + span, :]
    acc = jnp.broadcast_to(cb_ref[...], (tm, D_CONV))
    for t in range(CONV_K):
        off = HALO - CONV_K // 2 + t
        a, b = divmod(off, SUBLANES)
        if b == 0:
            tap = zs_ref[off:off + tm, :]
        else:
            tap = zsh_ref[b - 1, SUBLANES * a:SUBLANES * a + tm, :]
        acc = acc + cw_ref[t:t + 1, :] * tap
    mu = jnp.mean(acc, axis=-1, keepdims=True)
    cen = acc - mu
    var = jnp.mean(cen * cen, axis=-1, keepdims=True)
    yn = cen * lax.rsqrt(var + EPS) * cng_ref[...] + cnb_ref[...]
    ya = _dot((yn * jax.nn.sigmoid(yn)).astype(BF16), wco_ref[...])

    g0 = jax.nn.sigmoid(_dot(xn, win_ref[:, OFF_G:OFF_G + D_MODEL]) + bg_ref[:, 0:D_MODEL])
    mab = g0 * ya

    hs = _gelu_tanh(_dot(xn, win_ref[:, OFF_S:OFF_S + 2 * D_SG]))
    u = hs[:, :D_SG]
    v = _rms(hs[:, D_SG:], sgg_ref[...]).astype(BF16)
    chunks = []
    for c in range(tm // SG_CHUNK):
        cols = []
        for g in range(SG_GROUPS):
            vb = v[c * SG_CHUNK:(c + 1) * SG_CHUNK, g * LANES:(g + 1) * LANES]
            cols.append(_dot(wsp_ref[g], vb))
        chunks.append(jnp.concatenate(cols, axis=1) + bsp_ref[...])
    sv = jnp.concatenate(chunks, axis=0)
    yb = _dot((u * sv).astype(BF16), wso_ref[...])
    g1 = jax.nn.sigmoid(_dot(xn, win_ref[:, OFF_G + D_MODEL:OFF_G + 2 * D_MODEL])
                        + bg_ref[:, D_MODEL:2 * D_MODEL])
    mab_ref[...] = mab + g1 * yb

    g2 = jax.nn.sigmoid(_dot(xn, win_ref[:, OFF_G + 2 * D_MODEL:OFF_G + 3 * D_MODEL])
                        + bg_ref[:, 2 * D_MODEL:3 * D_MODEL])
    g2_ref[...] = g2.astype(g2_ref.dtype)

    cs = cs_ref[...]
    sn = sn_ref[...]

    def head_norm_rope(hraw, g):
        ms = jnp.sum(hraw * hraw, axis=-1, keepdims=True) * (1.0 / QK_HEAD)
        hn = hraw * lax.rsqrt(ms + EPS) * g
        return hn * cs + pltpu.roll(hn, HEAD_PAD // 2, 1) * sn

    ql = _rms(_dot(xn, win_ref[:, OFF_Q:OFF_Q + Q_LORA]), qng_ref[...]).astype(BF16)
    q_raw = _dot(ql, wuq_ref[...])
    gq = gq_ref[...]
    scale = math.log2(math.e) / math.sqrt(QK_HEAD)
    for h in range(N_HEADS):
        qh = head_norm_rope(q_raw[:, h * HEAD_PAD:(h + 1) * HEAD_PAD], gq)
        qt_ref[h * HEAD_PAD:(h + 1) * HEAD_PAD, :] = (qh * scale).T.astype(qt_ref.dtype)

    kvl = _rms(_dot(xn, win_ref[:, OFF_KV:OFF_KV + KV_LORA]), kvg_ref[...]).astype(BF16)
    k_nope = _dot(kvl, wuk_ref[...])
    k_pe = _dot(xn, win_ref[:, OFF_KPE:OFF_KPE + LANES])
    gk = gk_ref[...]
    for h in range(N_HEADS):
        kh = head_norm_rope(k_nope[:, h * HEAD_PAD:(h + 1) * HEAD_PAD] + k_pe, gk)
        k_ref[h] = kh.astype(k_ref.dtype)

    vv = _dot(kvl, wuv_ref[...])
    vt_ref[...] = vv.T.astype(vt_ref.dtype)


def _const_spec(shape):
    nd = len(shape)
    return pl.BlockSpec(shape, lambda b, j: (0,) * nd, pipeline_mode=pl.Buffered(1))


def _pre_call(x, rope_c, rope_s, p):
    B, S, D = x.shape
    tm = TM_PRE
    nj = S // tm
    hb = tm // HALO
    last_hb = S // HALO - 1
    in_specs = [
        pl.BlockSpec((None, tm, D), lambda b, j: (b, j, 0)),
        pl.BlockSpec((None, HALO, D), lambda b, j: (b, jnp.maximum(j * hb - 1, 0), 0)),
        pl.BlockSpec((None, HALO, D), lambda b, j: (b, jnp.minimum((j + 1) * hb, last_hb), 0)),
        pl.BlockSpec((tm, LANES), lambda b, j: (j, 0)),
        pl.BlockSpec((tm, LANES), lambda b, j: (j, 0)),
    ]
    consts = [p['ln1_g'], p['w_in'], p['b_gate'], p['conv_w'], p['conv_b'], p['conv_norm_g'],
              p['conv_norm_b'], p['w_conv_out'], p['sg_norm_g'], p['w_spatial'], p['b_spatial'],
              p['w_sg_out'], p['q_norm_g'], p['w_uq'], p['kv_norm_g'], p['w_uk'], p['w_uv'],
              p['g_q'], p['g_k']]
    in_specs += [_const_spec(c.shape) for c in consts]
    out_shape = [
        jax.ShapeDtypeStruct((B, S, D), F32),
        jax.ShapeDtypeStruct((B, S, D), BF16),
        jax.ShapeDtypeStruct((B, D_QK_PAD, S), BF16),
        jax.ShapeDtypeStruct((B, N_HEADS, S, HEAD_PAD), BF16),
        jax.ShapeDtypeStruct((B, N_HEADS * V_HEAD, S), BF16),
    ]
    out_specs = [
        pl.BlockSpec((None, tm, D), lambda b, j: (b, j, 0)),
        pl.BlockSpec((None, tm, D), lambda b, j: (b, j, 0)),
        pl.BlockSpec((None, D_QK_PAD, tm), lambda b, j: (b, 0, j)),
        pl.BlockSpec((None, N_HEADS, tm, HEAD_PAD), lambda b, j: (b, 0, j, 0)),
        pl.BlockSpec((None, N_HEADS * V_HEAD, tm), lambda b, j: (b, 0, j)),
    ]
    return pl.pallas_call(
        _pre_kernel,
        grid=(B, nj),
        in_specs=in_specs,
        out_specs=out_specs,
        out_shape=out_shape,
        scratch_shapes=[pltpu.VMEM((tm + 2 * HALO, D_CONV), F32),
                        pltpu.VMEM((SUBLANES - 1, tm + 2 * HALO - SUBLANES, D_CONV), F32)],
        compiler_params=pltpu.CompilerParams(
            dimension_semantics=("parallel", "parallel"), vmem_limit_bytes=VMEM_LIMIT),
        name="pre_mixer",
    )(x, x, x, rope_c, rope_s, *consts)


def _attn_kernel(qt_ref, k_ref, vt_ref, o_ref, s_scr, ot_scr):
    G, S = k_ref.shape[0], k_ref.shape[1]
    tq = qt_ref.shape[2]
    ck = min(ATTN_CHUNK, S)
    nck = S // ck
    ones = jnp.ones((HALO, ck), BF16)
    neg = jnp.full((8, tq), -jnp.inf, F32)

    def produce(h, c, m8):
        st = _dot(k_ref[h, c * ck:(c + 1) * ck, :], qt_ref[h])
        s_scr[c * ck:(c + 1) * ck, :] = st
        r = st
        while r.shape[0] > 8:
            half = r.shape[0] // 2
            r = jnp.maximum(r[:half], r[half:])
        return jnp.maximum(m8, r)

    def consume(h, c, m, acc):
        p = jnp.exp2(s_scr[c * ck:(c + 1) * ck, :] - m).astype(BF16)
        vaug = jnp.concatenate([vt_ref[h, :, c * ck:(c + 1) * ck], ones], axis=0)
        return acc + _dot(vaug, p)

    def finish(h, acc):
        row = pl.multiple_of(h * V_HEAD, V_HEAD)
        ot_scr[pl.ds(row, V_HEAD), :] = acc[:V_HEAD] / acc[V_HEAD:V_HEAD + 1]

    def row_max(m8):
        return jnp.max(m8, axis=0, keepdims=True)

    m8 = neg
    for c in range(nck):
        m8 = produce(0, c, m8)

    def body(h, m_prev):
        acc = jnp.zeros((V_HEAD + HALO, tq), F32)
        m8 = neg
        for c in range(nck):
            acc = consume(h - 1, c, m_prev, acc)
            m8 = produce(h, c, m8)
        finish(h - 1, acc)
        return row_max(m8)

    m_last = lax.fori_loop(1, G, body, row_max(m8))
    acc = jnp.zeros((V_HEAD + HALO, tq), F32)
    for c in range(nck):
        acc = consume(G - 1, c, m_last, acc)
    finish(G - 1, acc)
    o_ref[...] = ot_scr[...].T.astype(o_ref.dtype)


def _attn_call(qt, k, vt):
    B, H, S, _ = k.shape
    tq = TQ_ATTN
    G = ATTN_HEADS_PER_STEP
    assert S % min(ATTN_CHUNK, S) == 0 and S % tq == 0 and H % G == 0
    single = pl.Buffered(1)
    return pl.pallas_call(
        _attn_kernel,
        grid=(B, H // G, S // tq),
        in_specs=[
            pl.BlockSpec((None, G, HEAD_PAD, tq), lambda b, g, i: (b, g, 0, i)),
            pl.BlockSpec((None, G, S, HEAD_PAD), lambda b, g, i: (b, g, 0, 0), pipeline_mode=single),
            pl.BlockSpec((None, G, V_HEAD, S), lambda b, g, i: (b, g, 0, 0), pipeline_mode=single),
        ],
        out_specs=pl.BlockSpec((None, tq, G * V_HEAD), lambda b, g, i: (b, i, g)),
        out_shape=jax.ShapeDtypeStruct((B, S, H * V_HEAD), BF16),
        scratch_shapes=[pltpu.VMEM((S, tq), F32), pltpu.VMEM((G * V_HEAD, tq), F32)],
        compiler_params=pltpu.CompilerParams(
            dimension_semantics=("parallel", "parallel", "arbitrary"), vmem_limit_bytes=VMEM_LIMIT),
        name="attention",
    )(qt, k, vt)


def _post_kernel(x_ref, o_ref, mab_ref, g2_ref, wo_ref, wout_ref, ln2_ref, wfi_ref, wfo_ref, y_ref):
    yc = _dot(o_ref[...], wo_ref[...])
    merged = mab_ref[...] + g2_ref[...].astype(F32) * yc
    x1 = x_ref[...] + _dot(merged.astype(BF16), wout_ref[...])
    hn = _rms(x1, ln2_ref[...]).astype(BF16)
    f = _dot(hn, wfi_ref[...])
    f_in = f[:, :D_FF]
    f_gate = f[:, D_FF:]
    act = (f_gate * jax.nn.sigmoid(f_gate) * f_in).astype(BF16)
    y_ref[...] = x1 + _dot(act, wfo_ref[...])


def _post_call(x, o, mab, g2, p):
    T, D = x.shape
    tm = TM_POST
    consts = [p['w_o'], p['w_out'], p['ln2_g'], p['w_ffn_in'], p['w_ffn_out']]

    def cspec(shape):
        nd = len(shape)
        return pl.BlockSpec(shape, lambda i: (0,) * nd, pipeline_mode=pl.Buffered(1))

    return pl.pallas_call(
        _post_kernel,
        grid=(T // tm,),
        in_specs=[
            pl.BlockSpec((tm, D), lambda i: (i, 0)),
            pl.BlockSpec((tm, N_HEADS * V_HEAD), lambda i: (i, 0)),
            pl.BlockSpec((tm, D), lambda i: (i, 0)),
            pl.BlockSpec((tm, D), lambda i: (i, 0)),
        ] + [cspec(c.shape) for c in consts],
        out_specs=pl.BlockSpec((tm, D), lambda i: (i, 0)),
        out_shape=jax.ShapeDtypeStruct((T, D), F32),
        compiler_params=pltpu.CompilerParams(
            dimension_semantics=("parallel",), vmem_limit_bytes=VMEM_LIMIT),
        name="post_mixer",
    )(x, o, mab, g2, *consts)


def _prep_layer(l, ln1_g, w_in, b_gate, conv_w, conv_b, conv_norm_g, conv_norm_b, w_conv_out,
                sg_norm_g, w_spatial, b_spatial, w_sg_out, q_norm_g, w_uq, kv_norm_g, w_ukv,
                qk_q_g, qk_k_g, w_o, w_out, ln2_g, w_ffn_in, w_ffn_out):
    src = jnp.asarray(_HEAD_SRC)
    wi = w_in[l]
    d_in = wi.shape[1]
    kpe = wi[:, d_in - QK_ROPE:]
    kpe_pad = jnp.zeros((D_MODEL, LANES), F32)
    kpe_pad = kpe_pad.at[:, 0:_HALF].set(kpe[:, :_HALF])
    kpe_pad = kpe_pad.at[:, HEAD_PAD // 2:HEAD_PAD // 2 + _HALF].set(kpe[:, _HALF:])
    w_in_p = jnp.concatenate([wi[:, :d_in - QK_ROPE], kpe_pad], axis=1).astype(BF16)

    def pad_heads(w):
        w = jnp.pad(w, ((0, 0), (0, 0), (0, HEAD_PAD - QK_HEAD)))
        return w.reshape(w.shape[0], N_HEADS * HEAD_PAD)

    wuq = pad_heads(w_uq[l].reshape(Q_LORA, N_HEADS, QK_HEAD)[:, :, src]).astype(BF16)
    wkv = w_ukv[l].reshape(KV_LORA, N_HEADS, QK_NOPE + V_HEAD)
    wk = jnp.concatenate([wkv[:, :, :QK_NOPE], jnp.zeros((KV_LORA, N_HEADS, QK_ROPE), F32)], axis=2)
    wuk = pad_heads(wk[:, :, src]).astype(BF16)
    wuv = wkv[:, :, QK_NOPE:].reshape(KV_LORA, N_HEADS * V_HEAD).astype(BF16)

    def pad_gain(g):
        return jnp.pad(g[src], (0, HEAD_PAD - QK_HEAD)).reshape(1, HEAD_PAD)

    return {
        'ln1_g': ln1_g[l].reshape(1, -1),
        'w_in': w_in_p,
        'b_gate': b_gate[l].reshape(1, -1),
        'conv_w': conv_w[l],
        'conv_b': conv_b[l].reshape(1, -1),
        'conv_norm_g': conv_norm_g[l].reshape(1, -1),
        'conv_norm_b': conv_norm_b[l].reshape(1, -1),
        'w_conv_out': w_conv_out[l].astype(BF16),
        'sg_norm_g': sg_norm_g[l].reshape(1, -1),
        'w_spatial': w_spatial[l].astype(BF16),
        'b_spatial': jnp.repeat(b_spatial[l].T, D_SG // SG_GROUPS, axis=1),
        'w_sg_out': w_sg_out[l].astype(BF16),
        'q_norm_g': q_norm_g[l].reshape(1, -1),
        'w_uq': wuq,
        'kv_norm_g': kv_norm_g[l].reshape(1, -1),
        'w_uk': wuk,
        'w_uv': wuv,
        'g_q': pad_gain(qk_q_g[l]),
        'g_k': pad_gain(qk_k_g[l]),
        'w_o': w_o[l].astype(BF16),
        'w_out': w_out[l].astype(BF16),
        'ln2_g': ln2_g[l].reshape(1, -1),
        'w_ffn_in': w_ffn_in[l].astype(BF16),
        'w_ffn_out': w_ffn_out[l].astype(BF16),
    }


def _rope_tables(S):
    pos = jnp.arange(S, dtype=F32)
    inv = ROPE_BASE ** (-jnp.arange(0, QK_ROPE, 2, dtype=F32) / QK_ROPE)
    ang = pos[:, None] * inv[None, :]
    cos = jnp.cos(ang)
    sin = jnp.sin(ang)
    c = jnp.ones((S, HEAD_PAD), F32)
    c = c.at[:, 0:_HALF].set(cos).at[:, HEAD_PAD // 2:HEAD_PAD // 2 + _HALF].set(cos)
    s = jnp.zeros((S, HEAD_PAD), F32)
    s = s.at[:, 0:_HALF].set(-sin).at[:, HEAD_PAD // 2:HEAD_PAD // 2 + _HALF].set(sin)
    return c, s


def _trunk(x, layers):
    B, S, D = x.shape
    rope_c, rope_s = _rope_tables(S)
    for p in layers:
        mab, g2, qt, k, vt = _pre_call(x, rope_c, rope_s, p)
        o = _attn_call(qt.reshape(B, N_HEADS, HEAD_PAD, S), k, vt.reshape(B, N_HEADS, V_HEAD, S))
        x = _post_call(x.reshape(B * S, D), o.reshape(B * S, -1), mab.reshape(B * S, D),
                       g2.reshape(B * S, D), p).reshape(B, S, D)
    return x


def kernel(x_prompt, x_sample, ln1_g, w_in, b_gate, conv_w, conv_b, conv_norm_g, conv_norm_b, w_conv_out, sg_norm_g, w_spatial, b_spatial, w_sg_out, q_norm_g, w_uq, kv_norm_g, w_ukv, qk_q_g, qk_k_g, w_o, w_out, ln2_g, w_ffn_in, w_ffn_out):
    params = (ln1_g, w_in, b_gate, conv_w, conv_b, conv_norm_g, conv_norm_b, w_conv_out,
              sg_norm_g, w_spatial, b_spatial, w_sg_out, q_norm_g, w_uq, kv_norm_g, w_ukv,
              qk_q_g, qk_k_g, w_o, w_out, ln2_g, w_ffn_in, w_ffn_out)
    layers = [_prep_layer(l, *params) for l in range(ln1_g.shape[0])]
    return (_trunk(x_prompt, layers), _trunk(x_sample, layers))
```

```python
import math
from functools import partial

import numpy as np
import jax
import jax.numpy as jnp
from jax import lax
from jax.experimental import pallas as pl
from jax.experimental.pallas import tpu as pltpu

F32 = jnp.float32
BF16 = jnp.bfloat16

EPS = 1e-6
D_MODEL = 1024
D_CONV = 768
CONV_K = 31
D_SG = 768
SG_CHUNK = 128
SG_GROUPS = 6
N_HEADS = 12
QK_NOPE = 64
QK_ROPE = 32
QK_HEAD = QK_NOPE + QK_ROPE
V_HEAD = 64
Q_LORA = 256
KV_LORA = 128
ROPE_BASE = 10000.0
D_FF = 2816

LANES = 128
SUBLANES = 8
HEAD_PAD = LANES
HALO = 16
D_QK_PAD = N_HEADS * HEAD_PAD

OFF_G = 0
OFF_C = OFF_G + 3 * D_MODEL
OFF_S = OFF_C + 2 * D_CONV
OFF_Q = OFF_S + 2 * D_SG
OFF_KV = OFF_Q + Q_LORA
OFF_KPE = OFF_KV + KV_LORA
D_IN_PAD = OFF_KPE + LANES

TM_PRE = 256
TM_POST = 256
TQ_ATTN = 512
ATTN_CHUNK = 512
ATTN_HEADS_PER_STEP = 6
VMEM_LIMIT = 56 * 1024 * 1024

_HALF = QK_ROPE // 2
_HEAD_SRC = np.concatenate([
    np.arange(QK_NOPE, QK_NOPE + _HALF),
    np.arange(0, 48),
    np.arange(QK_NOPE + _HALF, QK_HEAD),
    np.arange(48, QK_NOPE),
])


def _rms(x, g):
    ms = jnp.mean(x * x, axis=-1, keepdims=True)
    return x * lax.rsqrt(ms + EPS) * g


def _gelu_tanh(x):
    c = math.sqrt(2.0 / math.pi)
    return 0.5 * x * (1.0 + jnp.tanh(c * (x + 0.044715 * (x * x * x))))


def _dot(a, b):
    return jnp.dot(a, b, preferred_element_type=F32)


def _dot_nt(a, b):
    return lax.dot_general(a, b, (((1,), (1,)), ((), ())), preferred_element_type=F32)


def _pre_kernel(x_ref, xp_ref, xn_ref, cs_ref, sn_ref, ln1_ref, win_ref, bg_ref, cw_ref,
                cb_ref, cng_ref, cnb_ref, wco_ref, sgg_ref, wsp_ref, bsp_ref, wso_ref,
                qng_ref, wuq_ref, kvg_ref, wuk_ref, wuv_ref, gq_ref, gk_ref,
                mab_ref, g2_ref, qt_ref, k_ref, vt_ref, zs_ref, zsh_ref):
    tm = x_ref.shape[0]
    j = pl.program_id(1)
    nj = pl.num_programs(1)
    ln1 = ln1_ref[...]

    xn = _rms(x_ref[...], ln1).astype(BF16)
    xh = jnp.concatenate([xp_ref[...], xn_ref[...]], axis=0)
    xhn = _rms(xh, ln1).astype(BF16)
    xe = jnp.concatenate([xhn[:HALO], xn, xhn[HALO:]], axis=0)

    hc = _dot(xe, win_ref[:, OFF_C:OFF_C + 2 * D_CONV])
    z = hc[:, :D_CONV] * jax.nn.sigmoid(hc[:, D_CONV:])
    row = lax.broadcasted_iota(jnp.int32, (tm + 2 * HALO, 1), 0)
    inside = jnp.logical_and(jnp.logical_or(row >= HALO, j > 0),
                             jnp.logical_or(row < tm + HALO, j < nj - 1))
    zs_ref[...] = jnp.where(inside, z, 0.0)
    span = tm + SUBLANES * (2 * HALO // SUBLANES - 1)
    for b in range(1, SUBLANES):
        zsh_ref[b - 1] = zs_ref[b:b + span, :]
    acc = jnp.broadcast_to(cb_ref[...], (tm, D_CONV))
    for t in range(CONV_K):
        off = HALO - CONV_K // 2 + t
        a, b = divmod(off, SUBLANES)
        if b == 0:
            tap = zs_ref[off:off + tm, :]
        else:
            tap = zsh_ref[b - 1, SUBLANES * a:SUBLANES * a + tm, :]
        acc = acc + cw_ref[t:t + 1, :] * tap
    mu = jnp.mean(acc, axis=-1, keepdims=True)
    cen = acc - mu
    var = jnp.mean(cen * cen, axis=-1, keepdims=True)
    yn = cen * lax.rsqrt(var + EPS) * cng_ref[...] + cnb_ref[...]
    ya = _dot((yn * jax.nn.sigmoid(yn)).astype(BF16), wco_ref[...])

    g0 = jax.nn.sigmoid(_dot(xn, win_ref[:, OFF_G:OFF_G + D_MODEL]) + bg_ref[:, 0:D_MODEL])
    mab = g0 * ya

    hs = _gelu_tanh(_dot(xn, win_ref[:, OFF_S:OFF_S + 2 * D_SG]))
    u = hs[:, :D_SG]
    v = _rms(hs[:, D_SG:], sgg_ref[...]).astype(BF16)
    chunks = []
    for c in range(tm // SG_CHUNK):
        cols = []
        for g in range(SG_GROUPS):
            vb = v[c * SG_CHUNK:(c + 1) * SG_CHUNK, g * LANES:(g + 1) * LANES]
            cols.append(_dot(wsp_ref[g], vb))
        chunks.append(jnp.concatenate(cols, axis=1) + bsp_ref[...])
    sv = jnp.concatenate(chunks, axis=0)
    yb = _dot((u * sv).astype(BF16), wso_ref[...])
    g1 = jax.nn.sigmoid(_dot(xn, win_ref[:, OFF_G + D_MODEL:OFF_G + 2 * D_MODEL])
                        + bg_ref[:, D_MODEL:2 * D_MODEL])
    mab_ref[...] = mab + g1 * yb

    g2 = jax.nn.sigmoid(_dot(xn, win_ref[:, OFF_G + 2 * D_MODEL:OFF_G + 3 * D_MODEL])
                        + bg_ref[:, 2 * D_MODEL:3 * D_MODEL])
    g2_ref[...] = g2.astype(g2_ref.dtype)

    cs = cs_ref[...]
    sn = sn_ref[...]

    def head_norm_rope(hraw, g):
        ms = jnp.sum(hraw * hraw, axis=-1, keepdims=True) * (1.0 / QK_HEAD)
        hn = hraw * lax.rsqrt(ms + EPS) * g
        return hn * cs + pltpu.roll(hn, HEAD_PAD // 2, 1) * sn

    ql = _rms(_dot(xn, win_ref[:, OFF_Q:OFF_Q + Q_LORA]), qng_ref[...]).astype(BF16)
    q_raw = _dot(ql, wuq_ref[...])
    gq = gq_ref[...]
    scale = math.log2(math.e) / math.sqrt(QK_HEAD)
    for h in range(N_HEADS):
        qh = head_norm_rope(q_raw[:, h * HEAD_PAD:(h + 1) * HEAD_PAD], gq)
        qt_ref[h * HEAD_PAD:(h + 1) * HEAD_PAD, :] = (qh * scale).T.astype(qt_ref.dtype)

    kvl = _rms(_dot(xn, win_ref[:, OFF_KV:OFF_KV + KV_LORA]), kvg_ref[...]).astype(BF16)
    k_nope = _dot(kvl, wuk_ref[...])
    k_pe = _dot(xn, win_ref[:, OFF_KPE:OFF_KPE + LANES])
    gk = gk_ref[...]
    for h in range(N_HEADS):
        kh = head_norm_rope(k_nope[:, h * HEAD_PAD:(h + 1) * HEAD_PAD] + k_pe, gk)
        k_ref[h] = kh.astype(k_ref.dtype)

    vv = _dot(kvl, wuv_ref[...])
    vt_ref[...] = vv.T.astype(vt_ref.dtype)


def _const_spec(shape):
    nd = len(shape)
    return pl.BlockSpec(shape, lambda b, j: (0,) * nd, pipeline_mode=pl.Buffered(1))


def _pre_call(x, rope_c, rope_s, p):
    B, S, D = x.shape
    tm = TM_PRE
    nj = S // tm
    hb = tm // HALO
    last_hb = S // HALO - 1
    in_specs = [
        pl.BlockSpec((None, tm, D), lambda b, j: (b, j, 0)),
        pl.BlockSpec((None, HALO, D), lambda b, j: (b, jnp.maximum(j * hb - 1, 0), 0)),
        pl.BlockSpec((None, HALO, D), lambda b, j: (b, jnp.minimum((j + 1) * hb, last_hb), 0)),
        pl.BlockSpec((tm, LANES), lambda b, j: (j, 0)),
        pl.BlockSpec((tm, LANES), lambda b, j: (j, 0)),
    ]
    consts = [p['ln1_g'], p['w_in'], p['b_gate'], p['conv_w'], p['conv_b'], p['conv_norm_g'],
              p['conv_norm_b'], p['w_conv_out'], p['sg_norm_g'], p['w_spatial'], p['b_spatial'],
              p['w_sg_out'], p['q_norm_g'], p['w_uq'], p['kv_norm_g'], p['w_uk'], p['w_uv'],
              p['g_q'], p['g_k']]
    in_specs += [_const_spec(c.shape) for c in consts]
    out_shape = [
        jax.ShapeDtypeStruct((B, S, D), F32),
        jax.ShapeDtypeStruct((B, S, D), BF16),
        jax.ShapeDtypeStruct((B, D_QK_PAD, S), BF16),
        jax.ShapeDtypeStruct((B, N_HEADS, S, HEAD_PAD), BF16),
        jax.ShapeDtypeStruct((B, N_HEADS * V_HEAD, S), BF16),
    ]
    out_specs = [
        pl.BlockSpec((None, tm, D), lambda b, j: (b, j, 0)),
        pl.BlockSpec((None, tm, D), lambda b, j: (b, j, 0)),
        pl.BlockSpec((None, D_QK_PAD, tm), lambda b, j: (b, 0, j)),
        pl.BlockSpec((None, N_HEADS, tm, HEAD_PAD), lambda b, j: (b, 0, j, 0)),
        pl.BlockSpec((None, N_HEADS * V_HEAD, tm), lambda b, j: (b, 0, j)),
    ]
    return pl.pallas_call(
        _pre_kernel,
        grid=(B, nj),
        in_specs=in_specs,
        out_specs=out_specs,
        out_shape=out_shape,
        scratch_shapes=[pltpu.VMEM((tm + 2 * HALO, D_CONV), F32),
                        pltpu.VMEM((SUBLANES - 1, tm + 2 * HALO - SUBLANES, D_CONV), F32)],
        compiler_params=pltpu.CompilerParams(
            dimension_semantics=("parallel", "parallel"), vmem_limit_bytes=VMEM_LIMIT),
        name="pre_mixer",
    )(x, x, x, rope_c, rope_s, *consts)


def _attn_kernel(qt_ref, k_ref, vt_ref, o_ref, s_scr, ot_scr):
    G, S = k_ref.shape[0], k_ref.shape[1]
    tq = qt_ref.shape[2]
    ck = min(ATTN_CHUNK, S)
    nck = S // ck
    ones = jnp.ones((HALO, ck), BF16)
    neg = jnp.full((8, tq), -jnp.inf, F32)

    def produce(h, c, m8):
        st = _dot(k_ref[h, c * ck:(c + 1) * ck, :], qt_ref[h])
        s_scr[c * ck:(c + 1) * ck, :] = st
        r = st
        while r.shape[0] > 8:
            half = r.shape[0] // 2
            r = jnp.maximum(r[:half], r[half:])
        return jnp.maximum(m8, r)

    def consume(h, c, m, acc):
        p = jnp.exp2(s_scr[c * ck:(c + 1) * ck, :] - m).astype(BF16)
        vaug = jnp.concatenate([vt_ref[h, :, c * ck:(c + 1) * ck], ones], axis=0)
        return acc + _dot(vaug, p)

    def finish(h, acc):
        row = pl.multiple_of(h * V_HEAD, V_HEAD)
        ot_scr[pl.ds(row, V_HEAD), :] = acc[:V_HEAD] / acc[V_HEAD:V_HEAD + 1]

    def row_max(m8):
        return jnp.max(m8, axis=0, keepdims=True)

    m8 = neg
    for c in range(nck):
        m8 = produce(0, c, m8)

    def body(h, m_prev):
        acc = jnp.zeros((V_HEAD + HALO, tq), F32)
        m8 = neg
        for c in range(nck):
            acc = consume(h - 1, c, m_prev, acc)
            m8 = produce(h, c, m8)
        finish(h - 1, acc)
        return row_max(m8)

    m_last = lax.fori_loop(1, G, body, row_max(m8))
    acc = jnp.zeros((V_HEAD + HALO, tq), F32)
    for c in range(nck):
        acc = consume(G - 1, c, m_last, acc)
    finish(G - 1, acc)
    o_ref[...] = ot_scr[...].T.astype(o_ref.dtype)


def _attn_call(qt, k, vt):
    B, H, S, _ = k.shape
    tq = TQ_ATTN
    G = ATTN_HEADS_PER_STEP
    assert S % min(ATTN_CHUNK, S) == 0 and S % tq == 0 and H % G == 0
    single = pl.Buffered(1)
    return pl.pallas_call(
        _attn_kernel,
        grid=(B, H // G, S // tq),
        in_specs=[
            pl.BlockSpec((None, G, HEAD_PAD, tq), lambda b, g, i: (b, g, 0, i)),
            pl.BlockSpec((None, G, S, HEAD_PAD), lambda b, g, i: (b, g, 0, 0), pipeline_mode=single),
            pl.BlockSpec((None, G, V_HEAD, S), lambda b, g, i: (b, g, 0, 0), pipeline_mode=single),
        ],
        out_specs=pl.BlockSpec((None, tq, G * V_HEAD), lambda b, g, i: (b, i, g)),
        out_shape=jax.ShapeDtypeStruct((B, S, H * V_HEAD), BF16),
        scratch_shapes=[pltpu.VMEM((S, tq), F32), pltpu.VMEM((G * V_HEAD, tq), F32)],
        compiler_params=pltpu.CompilerParams(
            dimension_semantics=("parallel", "parallel", "arbitrary"), vmem_limit_bytes=VMEM_LIMIT),
        name="attention",
    )(qt, k, vt)


def _post_kernel(x_ref, o_ref, mab_ref, g2_ref, wo_ref, wout_ref, ln2_ref, wfi_ref, wfo_ref, y_ref):
    yc = _dot(o_ref[...], wo_ref[...])
    merged = mab_ref[...] + g2_ref[...].astype(F32) * yc
    x1 = x_ref[...] + _dot(merged.astype(BF16), wout_ref[...])
    hn = _rms(x1, ln2_ref[...]).astype(BF16)
    f = _dot(hn, wfi_ref[...])
    f_in = f[:, :D_FF]
    f_gate = f[:, D_FF:]
    act = (f_gate * jax.nn.sigmoid(f_gate) * f_in).astype(BF16)
    y_ref[...] = x1 + _dot(act, wfo_ref[...])


def _post_call(x, o, mab, g2, p):
    T, D = x.shape
    tm = TM_POST
    consts = [p['w_o'], p['w_out'], p['ln2_g'], p['w_ffn_in'], p['w_ffn_out']]

    def cspec(shape):
        nd = len(shape)
        return pl.BlockSpec(shape, lambda i: (0,) * nd, pipeline_mode=pl.Buffered(1))

    return pl.pallas_call(
        _post_kernel,
        grid=(T // tm,),
        in_specs=[
            pl.BlockSpec((tm, D), lambda i: (i, 0)),
            pl.BlockSpec((tm, N_HEADS * V_HEAD), lambda i: (i, 0)),
            pl.BlockSpec((tm, D), lambda i: (i, 0)),
            pl.BlockSpec((tm, D), lambda i: (i, 0)),
        ] + [cspec(c.shape) for c in consts],
        out_specs=pl.BlockSpec((tm, D), lambda i: (i, 0)),
        out_shape=jax.ShapeDtypeStruct((T, D), F32),
        compiler_params=pltpu.CompilerParams(
            dimension_semantics=("parallel",), vmem_limit_bytes=VMEM_LIMIT),
        name="post_mixer",
    )(x, o, mab, g2, *consts)


def _prep_layer(l, ln1_g, w_in, b_gate, conv_w, conv_b, conv_norm_g, conv_norm_b, w_conv_out,
                sg_norm_g, w_spatial, b_spatial, w_sg_out, q_norm_g, w_uq, kv_norm_g, w_ukv,
                qk_q_g, qk_k_g, w_o, w_out, ln2_g, w_ffn_in, w_ffn_out):
    src = jnp.asarray(_HEAD_SRC)
    wi = w_in[l]
    d_in = wi.shape[1]
    kpe = wi[:, d_in - QK_ROPE:]
    kpe_pad = jnp.zeros((D_MODEL, LANES), F32)
    kpe_pad = kpe_pad.at[:, 0:_HALF].set(kpe[:, :_HALF])
    kpe_pad = kpe_pad.at[:, HEAD_PAD // 2:HEAD_PAD // 2 + _HALF].set(kpe[:, _HALF:])
    w_in_p = jnp.concatenate([wi[:, :d_in - QK_ROPE], kpe_pad], axis=1).astype(BF16)

    def pad_heads(w):
        w = jnp.pad(w, ((0, 0), (0, 0), (0, HEAD_PAD - QK_HEAD)))
        return w.reshape(w.shape[0], N_HEADS * HEAD_PAD)

    wuq = pad_heads(w_uq[l].reshape(Q_LORA, N_HEADS, QK_HEAD)[:, :, src]).astype(BF16)
    wkv = w_ukv[l].reshape(KV_LORA, N_HEADS, QK_NOPE + V_HEAD)
    wk = jnp.concatenate([wkv[:, :, :QK_NOPE], jnp.zeros((KV_LORA, N_HEADS, QK_ROPE), F32)], axis=2)
    wuk = pad_heads(wk[:, :, src]).astype(BF16)
    wuv = wkv[:, :, QK_NOPE:].reshape(KV_LORA, N_HEADS * V_HEAD).astype(BF16)

    def pad_gain(g):
        return jnp.pad(g[src], (0, HEAD_PAD - QK_HEAD)).reshape(1, HEAD_PAD)

    return {
        'ln1_g': ln1_g[l].reshape(1, -1),
        'w_in': w_in_p,
        'b_gate': b_gate[l].reshape(1, -1),
        'conv_w': conv_w[l],
        'conv_b': conv_b[l].reshape(1, -1),
        'conv_norm_g': conv_norm_g[l].reshape(1, -1),
        'conv_norm_b': conv_norm_b[l].reshape(1, -1),
        'w_conv_out': w_conv_out[l].astype(BF16),
        'sg_norm_g': sg_norm_g[l].reshape(1, -1),
        'w_spatial': w_spatial[l].astype(BF16),
        'b_spatial': jnp.repeat(b_spatial[l].T, D_SG // SG_GROUPS, axis=1),
        'w_sg_out': w_sg_out[l].astype(BF16),
        'q_norm_g': q_norm_g[l].reshape(1, -1),
        'w_uq': wuq,
        'kv_norm_g': kv_norm_g[l].reshape(1, -1),
        'w_uk': wuk,
        'w_uv': wuv,
        'g_q': pad_gain(qk_q_g[l]),
        'g_k': pad_gain(qk_k_g[l]),
        'w_o': w_o[l].astype(BF16),
        'w_out': w_out[l].astype(BF16),
        'ln2_g': ln2_g[l].reshape(1, -1),
        'w_ffn_in': w_ffn_in[l].astype(BF16),
        'w_ffn_out': w_ffn_out[l].astype(BF16),
    }


def _rope_tables(S):
    pos = jnp.arange(S, dtype=F32)
    inv = ROPE_BASE ** (-jnp.arange(0, QK_ROPE, 2, dtype=F32) / QK_ROPE)
    ang = pos[:, None] * inv[None, :]
    cos = jnp.cos(ang)
    sin = jnp.sin(ang)
    c = jnp.ones((S, HEAD_PAD), F32)
    c = c.at[:, 0:_HALF].set(cos).at[:, HEAD_PAD // 2:HEAD_PAD // 2 + _HALF].set(cos)
    s = jnp.zeros((S, HEAD_PAD), F32)
    s = s.at[:, 0:_HALF].set(-sin).at[:, HEAD_PAD // 2:HEAD_PAD // 2 + _HALF].set(sin)
    return c, s


def _trunk(x, layers):
    B, S, D = x.shape
    rope_c, rope_s = _rope_tables(S)
    for p in layers:
        mab, g2, qt, k, vt = _pre_call(x, rope_c, rope_s, p)
        o = _attn_call(qt.reshape(B, N_HEADS, HEAD_PAD, S), k, vt.reshape(B, N_HEADS, V_HEAD, S))
        x = _post_call(x.reshape(B * S, D), o.reshape(B * S, -1), mab.reshape(B * S, D),
                       g2.reshape(B * S, D), p).reshape(B, S, D)
    return x


def kernel(x_prompt, x_sample, ln1_g, w_in, b_gate, conv_w, conv_b, conv_norm_g, conv_norm_b, w_conv_out, sg_norm_g, w_spatial, b_spatial, w_sg_out, q_norm_g, w_uq, kv_norm_g, w_ukv, qk_q_g, qk_k_g, w_o, w_out, ln2_g, w_ffn_in, w_ffn_out):
    params = (ln1_g, w_in, b_gate, conv_w, conv_b, conv_norm_g, conv_norm_b, w_conv_out,
              sg_norm_g, w_spatial, b_spatial, w_sg_out, q_norm_g, w_uq, kv_norm_g, w_ukv,
              qk_q_g, qk_k_g, w_o, w_out, ln2_g, w_ffn_in, w_ffn_out)
    layers = [_prep_layer(l, *params) for l in range(ln1_g.shape[0])]
    return (_trunk(x_prompt, layers), _trunk(x_sample, layers))
```
